```python
import jax, jax.numpy as jnp
from jax import lax
import numpy as np

D_MODEL = 1024
BATCH = 16
SEQ = 4096
DEPTH = 4

N_MIXERS = 3
D_FF = 2816
LN_EPS = 1e-5
DEEPNORM_ALPHA = (2 * DEPTH) ** 0.25
DEEPNORM_BETA = (8 * DEPTH) ** -0.25
GMLP_CHUNK = 128
GMLP_WIDTH = 2 * D_MODEL
GMLP_GROUPS = 8
GMLP_GROUP_DIM = GMLP_WIDTH // GMLP_GROUPS
CONV_WIDTH = 31
CONV_DIM = D_MODEL
POOL_WINDOWS = (2, 4, 8, 16)
POOL_GROUPS = len(POOL_WINDOWS)
POOL_DIM = D_MODEL
POOL_GROUP_DIM = POOL_DIM // POOL_GROUPS
N_LAYERS_A = len(range(0, DEPTH, N_MIXERS))
N_LAYERS_B = len(range(1, DEPTH, N_MIXERS))
N_LAYERS_C = len(range(2, DEPTH, N_MIXERS))

kernel_name = "hybrid_gmlp_conformer_pool_deepnorm"


def layer_norm(x, g, b):
    xf = x.astype(jnp.float32)
    mean = jnp.mean(xf, axis=-1, keepdims=True)
    var = jnp.mean(jnp.square(xf - mean), axis=-1, keepdims=True)
    y = (xf - mean) * lax.rsqrt(var + LN_EPS) * g.astype(jnp.float32) + b.astype(jnp.float32)
    return y.astype(x.dtype)


def swiglu_ffn(x, w_gate, w_up, w_down):
    h = jax.nn.silu(x @ w_gate) * (x @ w_up)
    return h @ w_down


def gmlp_mixer(x, w_in, b_in, ln_g, ln_b, w_s, b_s, w_out):
    bsz, seq, _ = x.shape
    z = jax.nn.gelu(x @ w_in + b_in)
    u, v = jnp.split(z, 2, axis=-1)
    v = layer_norm(v, ln_g, ln_b)
    n_chunks = seq // GMLP_CHUNK
    v = v.reshape(bsz, n_chunks, GMLP_CHUNK, GMLP_GROUPS, GMLP_GROUP_DIM)
    causal = jnp.tril(jnp.ones((GMLP_CHUNK, GMLP_CHUNK), dtype=bool))
    w = jnp.where(causal[None], w_s, jnp.zeros_like(w_s))
    sv = jnp.einsum('gpq,bnqgc->bnpgc', w, v) + jnp.transpose(b_s)[:, :, None]
    gated = u * sv.reshape(bsz, seq, GMLP_WIDTH)
    return gated @ w_out


def conv_mixer(x, w_pw1, b_pw1, w_dw, b_dw, ln_g, ln_b, w_pw2, b_pw2):
    h = x @ w_pw1 + b_pw1
    a, g = jnp.split(h, 2, axis=-1)
    h = a * jax.nn.sigmoid(g)
    h = lax.conv_general_dilated(
        h, w_dw[:, None, :], window_strides=(1,),
        padding=[(CONV_WIDTH - 1, 0)],
        dimension_numbers=('NWC', 'WIO', 'NWC'),
        feature_group_count=CONV_DIM) + b_dw
    h = jax.nn.silu(layer_norm(h, ln_g, ln_b))
    return h @ w_pw2 + b_pw2


def pool_mixer(x, w_in, w_grp, scale, w_out):
    bsz, seq, _ = x.shape
    h = x @ w_in
    hf = h.astype(jnp.float32).reshape(bsz, seq, POOL_GROUPS, POOL_GROUP_DIM)
    cs = jnp.pad(jnp.cumsum(hf, axis=1), ((0, 0), (1, 0), (0, 0), (0, 0)))
    pos = jnp.arange(1, seq + 1, dtype=jnp.float32)
    pooled = []
    for gi, win in enumerate(POOL_WINDOWS):
        c = cs[:, :, gi]
        lagged = jnp.pad(c[:, :seq + 1 - win], ((0, 0), (win, 0), (0, 0)))
        window_sum = c[:, 1:] - lagged[:, 1:]
        count = jnp.minimum(pos, float(win))
        pooled.append(window_sum / count[None, :, None])
    p = (jnp.stack(pooled, axis=2) - hf).astype(x.dtype)
    y = jnp.einsum('bsgc,gcd->bsgd', p, w_grp).reshape(bsz, seq, POOL_DIM) * scale
    return y @ w_out


def setup_inputs(seed: int = 0) -> dict:
    key = jax.random.key(seed)
    ks = iter(jax.random.split(key, 32))

    def nrm(shape, scale):
        return scale * jax.random.normal(next(ks), shape, jnp.float32)

    beta = DEEPNORM_BETA
    e2 = 2 * GMLP_WIDTH
    return {
        "x": nrm((BATCH, SEQ, D_MODEL), 1.0),
        "ln_g": 1.0 + nrm((DEPTH, 3, D_MODEL), 0.02),
        "ln_b": nrm((DEPTH, 3, D_MODEL), 0.02),
        "ffn_w_gate": nrm((DEPTH, 2, D_MODEL, D_FF), D_MODEL ** -0.5),
        "ffn_w_up": nrm((DEPTH, 2, D_MODEL, D_FF), D_MODEL ** -0.5),
        "ffn_w_down": nrm((DEPTH, 2, D_FF, D_MODEL), beta * D_FF ** -0.5),
        "a_w_in": nrm((N_LAYERS_A, D_MODEL, e2), D_MODEL ** -0.5),
        "a_b_in": nrm((N_LAYERS_A, e2), 0.02),
        "a_ln_g": 1.0 + nrm((N_LAYERS_A, GMLP_WIDTH), 0.02),
        "a_ln_b": nrm((N_LAYERS_A, GMLP_WIDTH), 0.02),
        "a_w_s": nrm((N_LAYERS_A, GMLP_GROUPS, GMLP_CHUNK, GMLP_CHUNK), GMLP_CHUNK ** -0.5),
        "a_b_s": 1.0 + nrm((N_LAYERS_A, GMLP_GROUPS, GMLP_CHUNK), 0.1),
        "a_w_out": nrm((N_LAYERS_A, GMLP_WIDTH, D_MODEL), beta * GMLP_WIDTH ** -0.5),
        "b_w_pw1": nrm((N_LAYERS_B, D_MODEL, 2 * CONV_DIM), D_MODEL ** -0.5),
        "b_b_pw1": nrm((N_LAYERS_B, 2 * CONV_DIM), 0.02),
        "b_w_dw": nrm((N_LAYERS_B, CONV_WIDTH, CONV_DIM), CONV_WIDTH ** -0.5),
        "b_b_dw": nrm((N_LAYERS_B, CONV_DIM), 0.02),
        "b_ln_g": 1.0 + nrm((N_LAYERS_B, CONV_DIM), 0.02),
        "b_ln_b": nrm((N_LAYERS_B, CONV_DIM), 0.02),
        "b_w_pw2": nrm((N_LAYERS_B, CONV_DIM, D_MODEL), beta * CONV_DIM ** -0.5),
        "b_b_pw2": nrm((N_LAYERS_B, D_MODEL), 0.02),
        "c_w_in": nrm((N_LAYERS_C, D_MODEL, POOL_DIM), D_MODEL ** -0.5),
        "c_w_grp": nrm((N_LAYERS_C, POOL_GROUPS, POOL_GROUP_DIM, POOL_GROUP_DIM), POOL_GROUP_DIM ** -0.5),
        "c_scale": 1.0 + nrm((N_LAYERS_C, POOL_DIM), 0.1),
        "c_w_out": nrm((N_LAYERS_C, POOL_DIM, D_MODEL), beta * POOL_DIM ** -0.5),
    }


def reference(x, ln_g, ln_b, ffn_w_gate, ffn_w_up, ffn_w_down,
              a_w_in, a_b_in, a_ln_g, a_ln_b, a_w_s, a_b_s, a_w_out,
              b_w_pw1, b_b_pw1, b_w_dw, b_b_dw, b_ln_g, b_ln_b, b_w_pw2, b_b_pw2,
              c_w_in, c_w_grp, c_scale, c_w_out):
    alpha = DEEPNORM_ALPHA
    for i in range(DEPTH):
        m = i % N_MIXERS
        j = i // N_MIXERS
        f1 = swiglu_ffn(x, ffn_w_gate[i, 0], ffn_w_up[i, 0], ffn_w_down[i, 0])
        x = layer_norm(alpha * x + 0.5 * f1, ln_g[i, 0], ln_b[i, 0])
        if m == 0:
            h = gmlp_mixer(x, a_w_in[j], a_b_in[j], a_ln_g[j], a_ln_b[j],
                           a_w_s[j], a_b_s[j], a_w_out[j])
        elif m == 1:
            h = conv_mixer(x, b_w_pw1[j], b_b_pw1[j], b_w_dw[j], b_b_dw[j],
                           b_ln_g[j], b_ln_b[j], b_w_pw2[j], b_b_pw2[j])
        else:
            h = pool_mixer(x, c_w_in[j], c_w_grp[j], c_scale[j], c_w_out[j])
        x = layer_norm(alpha * x + h, ln_g[i, 1], ln_b[i, 1])
        f2 = swiglu_ffn(x, ffn_w_gate[i, 1], ffn_w_up[i, 1], ffn_w_down[i, 1])
        x = layer_norm(alpha * x + 0.5 * f2, ln_g[i, 2], ln_b[i, 2])
    return x
```

```python
import functools

import jax
import jax.numpy as jnp
from jax import lax
from jax.experimental import pallas as pl
from jax.experimental.pallas import tpu as pltpu

N_MIXERS = 3
LN_EPS = 1e-5
GMLP_CHUNK = 128
GMLP_GROUPS = 8
CONV_WIDTH = 31
POOL_WINDOWS = (2, 4, 8, 16)

V7X_SUBLANES = 8
V7X_MXU_DIM = 256
V7X_VMEM_BYTES = 64 * 1024 * 1024

BF16 = jnp.bfloat16
F32 = jnp.float32


def _dot(a, b):
    return jnp.dot(a, b, preferred_element_type=F32)


def _layer_norm(y, g, b):
    mean = jnp.mean(y, axis=-1, keepdims=True)
    d = y - mean
    var = jnp.mean(d * d, axis=-1, keepdims=True)
    return d * lax.rsqrt(var + LN_EPS) * g + b


def _ffn_kernel(x_ref, wgu_ref, wd_ref, g_ref, b_ref, o_ref, acc_ref, *, alpha):
    n_chunks, _, two_c = wgu_ref.shape
    c = two_c // 2
    x = x_ref[...]
    xb = x.astype(BF16)
    for k in range(n_chunks):
        gu = _dot(xb, wgu_ref[k])
        gate = gu[:, :c]
        h = (jax.nn.silu(gate) * gu[:, c:]).astype(BF16)
        d = _dot(h, wd_ref[k])
        if k == 0:
            acc_ref[...] = d
        else:
            acc_ref[...] += d
    y = alpha * x + 0.5 * acc_ref[...]
    o_ref[...] = _layer_norm(y, g_ref[...], b_ref[...])


def _gmlp_kernel(x_ref, win_ref, bin_ref, lng_ref, lnb_ref, ws_ref, bst_ref, wout_ref,
                 g_ref, b_ref, o_ref, vf_ref, vn_ref, acc_ref, *, alpha):
    tm = x_ref.shape[0]
    e = vn_ref.shape[1]
    gd = e // GMLP_GROUPS
    p = GMLP_CHUNK
    x = x_ref[...]
    xb = x.astype(BF16)

    for g in range(GMLP_GROUPS):
        lo = e + g * gd
        vf_ref[:, g * gd:(g + 1) * gd] = jax.nn.gelu(
            _dot(xb, win_ref[:, lo:lo + gd]) + bin_ref[:, lo:lo + gd])
    vn_ref[...] = _layer_norm(vf_ref[...], lng_ref[...], lnb_ref[...]).astype(BF16)

    causal = (lax.broadcasted_iota(jnp.int32, (p, p), 0)
              >= lax.broadcasted_iota(jnp.int32, (p, p), 1))
    for g in range(GMLP_GROUPS):
        lo = g * gd
        u = jax.nn.gelu(_dot(xb, win_ref[:, lo:lo + gd]) + bin_ref[:, lo:lo + gd])
        ws = jnp.where(causal, ws_ref[g], jnp.zeros_like(ws_ref[g]))
        bs = bst_ref[:, g:g + 1]
        parts = []
        for n in range(tm // p):
            sv = _dot(ws, vn_ref[n * p:(n + 1) * p, lo:lo + gd]) + bs
            parts.append(u[n * p:(n + 1) * p, :] * sv)
        gated = jnp.concatenate(parts, axis=0).astype(BF16)
        d = _dot(gated, wout_ref[lo:lo + gd, :])
        if g == 0:
            acc_ref[...] = d
        else:
            acc_ref[...] += d
    y = alpha * x + acc_ref[...]
    o_ref[...] = _layer_norm(y, g_ref[...], b_ref[...])


CONV_HALO = 32
CONV_ROW_BLOCK = 32


def _conv_kernel(x_ref, w1_ref, b1_ref, wdw_ref, bdw_ref, lng_ref, lnb_ref, w2_ref, b2_ref,
                 g_ref, b_ref, o_ref, hbuf_ref, conv_ref, *, alpha, tiles_per_seq):
    tm = x_ref.shape[0]
    c = conv_ref.shape[1]
    i = pl.program_id(0)

    @pl.when(i % tiles_per_seq == 0)
    def _():
        hbuf_ref[0:CONV_HALO, :] = jnp.zeros((CONV_HALO, c), F32)

    x = x_ref[...]
    xb = x.astype(BF16)
    a = _dot(xb, w1_ref[:, :c]) + b1_ref[:, :c]
    gate = _dot(xb, w1_ref[:, c:]) + b1_ref[:, c:]
    hbuf_ref[CONV_HALO:CONV_HALO + tm, :] = a * jax.nn.sigmoid(gate)

    off = CONV_HALO - (CONV_WIDTH - 1)
    rb = CONV_ROW_BLOCK

    for r0 in range(0, tm, rb):
        acc = jnp.broadcast_to(bdw_ref[...], (rb, c))
        for k in range(CONV_WIDTH):
            win = hbuf_ref[r0 + off + k:r0 + off + k + rb, :]
            acc = acc + win * wdw_ref[k:k + 1, :]
        conv_ref[r0:r0 + rb, :] = acc
    hbuf_ref[0:CONV_HALO, :] = hbuf_ref[tm:tm + CONV_HALO, :]

    hc = jax.nn.silu(_layer_norm(conv_ref[...], lng_ref[...], lnb_ref[...]))
    out = _dot(hc.astype(BF16), w2_ref[...]) + b2_ref[...]
    y = alpha * x + out
    o_ref[...] = _layer_norm(y, g_ref[...], b_ref[...])


POOL_HALO = 16


def _pool_kernel(x_ref, win_ref, wgrp_ref, scale_ref, wout_ref, g_ref, b_ref, o_ref,
                 hbuf_ref, *, alpha, tiles_per_seq):
    tm = x_ref.shape[0]
    dp = hbuf_ref.shape[1]
    gd = dp // len(POOL_WINDOWS)
    i = pl.program_id(0)

    @pl.when(i % tiles_per_seq == 0)
    def _():
        hbuf_ref[0:POOL_HALO, :] = jnp.zeros((POOL_HALO, dp), F32)

    x = x_ref[...]
    xb = x.astype(BF16)
    hbuf_ref[POOL_HALO:POOL_HALO + tm, :] = _dot(xb, win_ref[...])

    pos = (i % tiles_per_seq) * tm + 1 + lax.broadcasted_iota(jnp.int32, (tm, gd), 0)
    ys = []
    for gi, win in enumerate(POOL_WINDOWS):
        lo = gi * gd
        rows = tm + POOL_HALO
        s = hbuf_ref[:, lo:lo + gd]
        w = 1
        while w < win:
            s = s[w:, :] + s[:rows - w, :]
            rows -= w
            w *= 2
        s = s[rows - tm:, :]
        h = hbuf_ref[POOL_HALO:POOL_HALO + tm, lo:lo + gd]
        count = jnp.minimum(pos, win).astype(F32)
        pooled = (s / count - h).astype(BF16)
        ys.append(_dot(pooled, wgrp_ref[gi]))
    hbuf_ref[0:POOL_HALO, :] = hbuf_ref[tm:tm + POOL_HALO, :]
    y = (jnp.concatenate(ys, axis=1) * scale_ref[...]).astype(BF16)
    out = _dot(y, wout_ref[...])
    o_ref[...] = _layer_norm(alpha * x + out, g_ref[...], b_ref[...])


def _resident(arr):
    zeros = (0,) * arr.ndim
    return pl.BlockSpec(arr.shape, lambda i: zeros, pipeline_mode=pl.Buffered(1))


def _token_tiled_call(kernel_fn, x2, params, scratch_shapes, *, tm, semantics, name):
    n_tok, d = x2.shape
    tile = pl.BlockSpec((tm, d), lambda i: (i, 0))
    return pl.pallas_call(
        kernel_fn,
        out_shape=jax.ShapeDtypeStruct((n_tok, d), x2.dtype),
        grid=(n_tok // tm,),
        in_specs=[tile] + [_resident(p) for p in params],
        out_specs=tile,
        scratch_shapes=scratch_shapes,
        compiler_params=pltpu.CompilerParams(
            dimension_semantics=(semantics,),
            vmem_limit_bytes=V7X_VMEM_BYTES * 7 // 8),
        name=name,
    )(x2, *params)


def _row(v):
    return v.reshape(1, -1).astype(F32)


def _pick_tile(seq, target):
    tm = min(seq, target)
    assert seq % tm == 0 and tm % GMLP_CHUNK == 0, (seq, tm)
    return tm


def _forward(x, ln_g, ln_b, ffn_w_gate, ffn_w_up, ffn_w_down,
             a_w_in, a_b_in, a_ln_g, a_ln_b, a_w_s, a_b_s, a_w_out,
             b_w_pw1, b_b_pw1, b_w_dw, b_b_dw, b_ln_g, b_ln_b, b_w_pw2, b_b_pw2,
             c_w_in, c_w_grp, c_scale, c_w_out, *, tile_target):
    bsz, seq, d = x.shape
    depth = ln_g.shape[0]
    d_ff = ffn_w_gate.shape[-1]
    alpha = float((2 * depth) ** 0.25)
    tm = _pick_tile(seq, tile_target)
    tiles_per_seq = seq // tm
    x2 = x.reshape(bsz * seq, d)

    chunk = V7X_MXU_DIM
    assert d_ff % chunk == 0, d_ff
    n_chunks = d_ff // chunk

    def ffn(x2, i, s):
        wg = ffn_w_gate[i, s].reshape(d, n_chunks, chunk)
        wu = ffn_w_up[i, s].reshape(d, n_chunks, chunk)
        wgu = jnp.concatenate([wg, wu], axis=2).transpose(1, 0, 2).astype(BF16)
        wd = ffn_w_down[i, s].reshape(n_chunks, chunk, d).astype(BF16)
        return _token_tiled_call(
            functools.partial(_ffn_kernel, alpha=alpha), x2,
            [wgu, wd, _row(ln_g[i, 2 * s]), _row(ln_b[i, 2 * s])],
            [pltpu.VMEM((tm, d), F32)],
            tm=tm, semantics="parallel", name=f"ffn_{i}_{s}")

    for i in range(depth):
        m, j = i % N_MIXERS, i // N_MIXERS
        x2 = ffn(x2, i, 0)
        g1, b1 = _row(ln_g[i, 1]), _row(ln_b[i, 1])
        if m == 0:
            e = a_w_out.shape[1]
            x2 = _token_tiled_call(
                functools.partial(_gmlp_kernel, alpha=alpha), x2,
                [a_w_in[j].astype(BF16), _row(a_b_in[j]), _row(a_ln_g[j]), _row(a_ln_b[j]),
                 a_w_s[j].astype(BF16), a_b_s[j].T.astype(F32), a_w_out[j].astype(BF16), g1, b1],
                [pltpu.VMEM((tm, e), F32), pltpu.VMEM((tm, e), BF16), pltpu.VMEM((tm, d), F32)],
                tm=tm, semantics="parallel", name=f"gmlp_{i}")
        elif m == 1:
            c = b_w_dw.shape[-1]
            x2 = _token_tiled_call(
                functools.partial(_conv_kernel, alpha=alpha, tiles_per_seq=tiles_per_seq), x2,
                [b_w_pw1[j].astype(BF16), _row(b_b_pw1[j]), b_w_dw[j].astype(F32), _row(b_b_dw[j]),
                 _row(b_ln_g[j]), _row(b_ln_b[j]), b_w_pw2[j].astype(BF16), _row(b_b_pw2[j]), g1, b1],
                [pltpu.VMEM((tm + CONV_HALO, c), F32), pltpu.VMEM((tm, c), F32)],
                tm=tm, semantics="arbitrary", name=f"conv_{i}")
        else:
            dp = c_w_in.shape[-1]
            x2 = _token_tiled_call(
                functools.partial(_pool_kernel, alpha=alpha, tiles_per_seq=tiles_per_seq), x2,
                [c_w_in[j].astype(BF16), c_w_grp[j].astype(BF16), _row(c_scale[j]),
                 c_w_out[j].astype(BF16), g1, b1],
                [pltpu.VMEM((tm + POOL_HALO, dp), F32)],
                tm=tm, semantics="arbitrary", name=f"pool_{i}")
        x2 = ffn(x2, i, 1)
    return x2.reshape(bsz, seq, d)


TOKEN_TILE = 512


def kernel(x, ln_g, ln_b, ffn_w_gate, ffn_w_up, ffn_w_down, a_w_in, a_b_in, a_ln_g, a_ln_b, a_w_s, a_b_s, a_w_out, b_w_pw1, b_b_pw1, b_w_dw, b_b_dw, b_ln_g, b_ln_b, b_w_pw2, b_b_pw2, c_w_in, c_w_grp, c_scale, c_w_out):
    return _forward(x, ln_g, ln_b, ffn_w_gate, ffn_w_up, ffn_w_down,
                    a_w_in, a_b_in, a_ln_g, a_ln_b, a_w_s, a_b_s, a_w_out,
                    b_w_pw1, b_b_pw1, b_w_dw, b_b_dw, b_ln_g, b_ln_b, b_w_pw2, b_b_pw2,
                    c_w_in, c_w_grp, c_scale, c_w_out, tile_target=TOKEN_TILE)
```

```python
import functools

import jax
import jax.numpy as jnp
from jax import lax
from jax.experimental import pallas as pl
from jax.experimental.pallas import tpu as pltpu

N_MIXERS = 3
LN_EPS = 1e-5
GMLP_CHUNK = 128
GMLP_GROUPS = 8
CONV_WIDTH = 31
POOL_WINDOWS = (2, 4, 8, 16)

V7X_SUBLANES = 8
V7X_MXU_DIM = 256
V7X_VMEM_BYTES = 64 * 1024 * 1024

BF16 = jnp.bfloat16
F32 = jnp.float32


def _dot(a, b):
    return jnp.dot(a, b, preferred_element_type=F32)


def _layer_norm(y, g, b):
    mean = jnp.mean(y, axis=-1, keepdims=True)
    d = y - mean
    var = jnp.mean(d * d, axis=-1, keepdims=True)
    return d * lax.rsqrt(var + LN_EPS) * g + b


def _ffn_kernel(x_ref, wg_ref, wu_ref, wd_ref, g_ref, b_ref, o_ref, acc_ref, *, alpha, sub_rows):
    d_ff = wg_ref.shape[1]
    c = V7X_MXU_DIM
    for r0 in range(0, x_ref.shape[0], sub_rows):
        rows = slice(r0, r0 + sub_rows)
        x = x_ref[rows, :]
        xb = x.astype(BF16)
        for k in range(d_ff // c):
            lo = k * c
            gate = _dot(xb, wg_ref[:, lo:lo + c])
            up = _dot(xb, wu_ref[:, lo:lo + c])
            h = (jax.nn.silu(gate) * up).astype(BF16)
            d = _dot(h, wd_ref[lo:lo + c, :])
            if k == 0:
                acc_ref[rows, :] = d
            else:
                acc_ref[rows, :] += d
        y = alpha * x + 0.5 * acc_ref[rows, :]
        o_ref[rows, :] = _layer_norm(y, g_ref[...], b_ref[...])


def _gmlp_kernel(x_ref, win_ref, bin_ref, lng_ref, lnb_ref, ws_ref, bst_ref, wout_ref,
                 g_ref, b_ref, o_ref, uf_ref, vf_ref, acc_ref, *, alpha, sub_rows):
    e = vf_ref.shape[1]
    gd = e // GMLP_GROUPS
    p = GMLP_CHUNK
    causal = (lax.broadcasted_iota(jnp.int32, (p, p), 0)
              >= lax.broadcasted_iota(jnp.int32, (p, p), 1))
    for r0 in range(0, x_ref.shape[0], sub_rows):
        rows = slice(r0, r0 + sub_rows)
        x = x_ref[rows, :]
        xb = x.astype(BF16)

        vsum = jnp.zeros((sub_rows, 1), F32)
        for g in range(GMLP_GROUPS):
            cols = slice(g * gd, (g + 1) * gd)
            vcols = slice(e + g * gd, e + (g + 1) * gd)
            v = jax.nn.gelu(_dot(xb, win_ref[:, vcols]) + bin_ref[:, vcols])
            vf_ref[rows, cols] = v
            vsum = vsum + jnp.sum(v, axis=-1, keepdims=True)
            uf_ref[rows, cols] = jax.nn.gelu(_dot(xb, win_ref[:, cols]) + bin_ref[:, cols])

        mean = vsum / e
        dv = vf_ref[rows, :] - mean
        rstd = lax.rsqrt(jnp.mean(dv * dv, axis=-1, keepdims=True) + LN_EPS)

        for g in range(GMLP_GROUPS):
            cols = slice(g * gd, (g + 1) * gd)
            vn = ((vf_ref[rows, cols] - mean) * rstd * lng_ref[:, cols] + lnb_ref[:, cols]).astype(BF16)
            ws = jnp.where(causal, ws_ref[g], jnp.zeros_like(ws_ref[g]))
            bs = bst_ref[:, g:g + 1]
            parts = []
            for n in range(sub_rows // p):
                sv = _dot(ws, vn[n * p:(n + 1) * p, :]) + bs
                parts.append(uf_ref[r0 + n * p:r0 + (n + 1) * p, cols] * sv)
            gated = jnp.concatenate(parts, axis=0).astype(BF16)
            d = _dot(gated, wout_ref[cols, :])
            if g == 0:
                acc_ref[rows, :] = d
            else:
                acc_ref[rows, :] += d
        y = alpha * x + acc_ref[rows, :]
        o_ref[rows, :] = _layer_norm(y, g_ref[...], b_ref[...])


CONV_HALO = 32
CONV_ROW_BLOCK = 32


def _depthwise_block(hs_ref, wb_ref, conv_ref, r0):
    sub = V7X_SUBLANES
    groups = range(CONV_ROW_BLOCK // sub)
    acc = [wb_ref[CONV_WIDTH] for _ in groups]
    for r in range(sub):
        for a in range((CONV_WIDTH - 1 - r) // sub + 1):
            w = wb_ref[CONV_WIDTH - 1 - (sub * a + r)]
            for q in groups:
                rows = hs_ref[r, pl.ds(r0 + (CONV_HALO - sub * a + sub * q), sub), :]
                acc[q] = acc[q] + rows * w
    for q in groups:
        conv_ref[pl.ds(r0 + sub * q, sub), :] = acc[q]


def _conv_kernel(x_ref, w1_ref, b1_ref, wdw_ref, bdw_ref, lng_ref, lnb_ref, w2_ref, b2_ref,
                 g_ref, b_ref, o_ref, hs_ref, wb_ref, conv_ref, *, alpha, tiles_per_seq):
    tm = x_ref.shape[0]
    c = conv_ref.shape[1]
    sub = V7X_SUBLANES
    i = pl.program_id(0)

    @pl.when(i % tiles_per_seq == 0)
    def _():
        for r in range(sub):
            hs_ref[r, 0:CONV_HALO + r, :] = jnp.zeros((CONV_HALO + r, c), F32)

    for k in range(CONV_WIDTH):
        wb_ref[k] = jnp.broadcast_to(wdw_ref[k:k + 1, :], (sub, c))
    wb_ref[CONV_WIDTH] = jnp.broadcast_to(bdw_ref[...], (sub, c))

    x = x_ref[...]
    xb = x.astype(BF16)
    a = _dot(xb, w1_ref[:, :c]) + b1_ref[:, :c]
    gate = _dot(xb, w1_ref[:, c:]) + b1_ref[:, c:]
    glu = a * jax.nn.sigmoid(gate)
    for r in range(sub):
        hs_ref[r, CONV_HALO + r:CONV_HALO + r + tm, :] = glu

    def conv_rows(j, carry):
        _depthwise_block(hs_ref, wb_ref, conv_ref, pl.multiple_of(j * CONV_ROW_BLOCK, CONV_ROW_BLOCK))
        return carry

    lax.fori_loop(0, tm // CONV_ROW_BLOCK, conv_rows, 0)
    for r in range(sub):
        hs_ref[r, 0:CONV_HALO + r, :] = hs_ref[r, tm:tm + CONV_HALO + r, :]

    hc = jax.nn.silu(_layer_norm(conv_ref[...], lng_ref[...], lnb_ref[...]))
    out = _dot(hc.astype(BF16), w2_ref[...]) + b2_ref[...]
    y = alpha * x + out
    o_ref[...] = _layer_norm(y, g_ref[...], b_ref[...])


POOL_HALO = 16


def _pool_kernel(x_ref, win_ref, wgrp_ref, scale_ref, wout_ref, g_ref, b_ref, o_ref,
                 hbuf_ref, *, alpha, tiles_per_seq):
    tm = x_ref.shape[0]
    dp = hbuf_ref.shape[1]
    gd = dp // len(POOL_WINDOWS)
    i = pl.program_id(0)

    @pl.when(i % tiles_per_seq == 0)
    def _():
        hbuf_ref[0:POOL_HALO, :] = jnp.zeros((POOL_HALO, dp), F32)

    x = x_ref[...]
    xb = x.astype(BF16)
    hbuf_ref[POOL_HALO:POOL_HALO + tm, :] = _dot(xb, win_ref[...])

    pos = (i % tiles_per_seq) * tm + 1 + lax.broadcasted_iota(jnp.int32, (tm, gd), 0)
    ys = []
    for gi, win in enumerate(POOL_WINDOWS):
        lo = gi * gd
        rows = tm + POOL_HALO
        s = hbuf_ref[:, lo:lo + gd]
        w = 1
        while w < win:
            s = s[w:, :] + s[:rows - w, :]
            rows -= w
            w *= 2
        s = s[rows - tm:, :]
        h = hbuf_ref[POOL_HALO:POOL_HALO + tm, lo:lo + gd]
        count = jnp.minimum(pos, win).astype(F32)
        pooled = (s / count - h).astype(BF16)
        ys.append(_dot(pooled, wgrp_ref[gi]))
    hbuf_ref[0:POOL_HALO, :] = hbuf_ref[tm:tm + POOL_HALO, :]
    y = (jnp.concatenate(ys, axis=1) * scale_ref[...]).astype(BF16)
    out = _dot(y, wout_ref[...])
    o_ref[...] = _layer_norm(alpha * x + out, g_ref[...], b_ref[...])


def _resident(arr):
    zeros = (0,) * arr.ndim
    return pl.BlockSpec(arr.shape, lambda i: zeros, pipeline_mode=pl.Buffered(1))


def _token_tiled_call(kernel_fn, x2, params, scratch_shapes, *, tm, semantics, name):
    n_tok, d = x2.shape
    tile = pl.BlockSpec((tm, d), lambda i: (i, 0))
    return pl.pallas_call(
        kernel_fn,
        out_shape=jax.ShapeDtypeStruct((n_tok, d), x2.dtype),
        grid=(n_tok // tm,),
        in_specs=[tile] + [_resident(p) for p in params],
        out_specs=tile,
        scratch_shapes=scratch_shapes,
        compiler_params=pltpu.CompilerParams(
            dimension_semantics=(semantics,),
            vmem_limit_bytes=V7X_VMEM_BYTES * 7 // 8),
        name=name,
    )(x2, *params)


def _row(v):
    return v.reshape(1, -1).astype(F32)


def _pick_tile(seq, target):
    tm = min(seq, target)
    assert seq % tm == 0 and tm % GMLP_CHUNK == 0, (seq, tm)
    return tm


FFN_TILE = 1024
FFN_SUB_TILE = 512
GMLP_TILE = 1024
GMLP_SUB_TILE = 512
MIXER_TILE = 512


def _forward(x, ln_g, ln_b, ffn_w_gate, ffn_w_up, ffn_w_down,
             a_w_in, a_b_in, a_ln_g, a_ln_b, a_w_s, a_b_s, a_w_out,
             b_w_pw1, b_b_pw1, b_w_dw, b_b_dw, b_ln_g, b_ln_b, b_w_pw2, b_b_pw2,
             c_w_in, c_w_grp, c_scale, c_w_out,
             *, ffn_tile=FFN_TILE, gmlp_tile=GMLP_TILE, mixer_tile=MIXER_TILE):
    bsz, seq, d = x.shape
    depth = ln_g.shape[0]
    d_ff = ffn_w_gate.shape[-1]
    alpha = float((2 * depth) ** 0.25)
    tm = _pick_tile(seq, mixer_tile)
    tm_ffn = _pick_tile(seq, ffn_tile)
    tm_gmlp = _pick_tile(seq, gmlp_tile)
    tiles_per_seq = seq // tm
    x2 = x.reshape(bsz * seq, d)

    assert d_ff % V7X_MXU_DIM == 0, d_ff

    def ffn(x2, i, s):
        return _token_tiled_call(
            functools.partial(_ffn_kernel, alpha=alpha, sub_rows=min(tm_ffn, FFN_SUB_TILE)), x2,
            [ffn_w_gate[i, s].astype(BF16), ffn_w_up[i, s].astype(BF16),
             ffn_w_down[i, s].astype(BF16), _row(ln_g[i, 2 * s]), _row(ln_b[i, 2 * s])],
            [pltpu.VMEM((tm_ffn, d), F32)],
            tm=tm_ffn, semantics="parallel", name="ffn")

    for i in range(depth):
        m, j = i % N_MIXERS, i // N_MIXERS
        x2 = ffn(x2, i, 0)
        g1, b1 = _row(ln_g[i, 1]), _row(ln_b[i, 1])
        if m == 0:
            e = a_w_out.shape[1]
            x2 = _token_tiled_call(
                functools.partial(_gmlp_kernel, alpha=alpha,
                                  sub_rows=min(tm_gmlp, GMLP_SUB_TILE)), x2,
                [a_w_in[j].astype(BF16), _row(a_b_in[j]), _row(a_ln_g[j]), _row(a_ln_b[j]),
                 a_w_s[j].astype(BF16), a_b_s[j].T.astype(F32), a_w_out[j].astype(BF16), g1, b1],
                [pltpu.VMEM((tm_gmlp, e), F32), pltpu.VMEM((tm_gmlp, e), F32),
                 pltpu.VMEM((tm_gmlp, d), F32)],
                tm=tm_gmlp, semantics="parallel", name="gmlp")
        elif m == 1:
            c = b_w_dw.shape[-1]
            x2 = _token_tiled_call(
                functools.partial(_conv_kernel, alpha=alpha, tiles_per_seq=tiles_per_seq), x2,
                [b_w_pw1[j].astype(BF16), _row(b_b_pw1[j]), b_w_dw[j].astype(F32), _row(b_b_dw[j]),
                 _row(b_ln_g[j]), _row(b_ln_b[j]), b_w_pw2[j].astype(BF16), _row(b_b_pw2[j]), g1, b1],
                [pltpu.VMEM((V7X_SUBLANES, tm + CONV_HALO + V7X_SUBLANES, c), F32),
                 pltpu.VMEM((CONV_WIDTH + 1, V7X_SUBLANES, c), F32), pltpu.VMEM((tm, c), F32)],
                tm=tm, semantics="arbitrary", name="conv")
        else:
            dp = c_w_in.shape[-1]
            x2 = _token_tiled_call(
                functools.partial(_pool_kernel, alpha=alpha, tiles_per_seq=tiles_per_seq), x2,
                [c_w_in[j].astype(BF16), c_w_grp[j].astype(BF16), _row(c_scale[j]),
                 c_w_out[j].astype(BF16), g1, b1],
                [pltpu.VMEM((tm + POOL_HALO, dp), F32)],
                tm=tm, semantics="arbitrary", name="pool")
        x2 = ffn(x2, i, 1)
    return x2.reshape(bsz, seq, d)


def kernel(x, ln_g, ln_b, ffn_w_gate, ffn_w_up, ffn_w_down, a_w_in, a_b_in, a_ln_g, a_ln_b, a_w_s, a_b_s, a_w_out, b_w_pw1, b_b_pw1, b_w_dw, b_b_dw, b_ln_g, b_ln_b, b_w_pw2, b_b_pw2, c_w_in, c_w_grp, c_scale, c_w_out):
    return _forward(x, ln_g, ln_b, ffn_w_gate, ffn_w_up, ffn_w_down,
                    a_w_in, a_b_in, a_ln_g, a_ln_b, a_w_s, a_b_s, a_w_out,
                    b_w_pw1, b_b_pw1, b_w_dw, b_b_dw, b_ln_g, b_ln_b, b_w_pw2, b_b_pw2,
                    c_w_in, c_w_grp, c_scale, c_w_out)
```

```python
import functools

import jax
import jax.numpy as jnp
from jax import lax
from jax.experimental import pallas as pl
from jax.experimental.pallas import tpu as pltpu

N_MIXERS = 3
LN_EPS = 1e-5
GMLP_CHUNK = 128
GMLP_GROUPS = 8
CONV_WIDTH = 31
POOL_WINDOWS = (2, 4, 8, 16)

V7X_SUBLANES = 8
V7X_MXU_DIM = 256
V7X_VMEM_BYTES = 64 * 1024 * 1024

BF16 = jnp.bfloat16
F32 = jnp.float32


def _dot(a, b):
    return jnp.dot(a, b, preferred_element_type=F32)


def _layer_norm(y, g, b):
    mean = jnp.mean(y, axis=-1, keepdims=True)
    d = y - mean
    var = jnp.mean(d * d, axis=-1, keepdims=True)
    return d * lax.rsqrt(var + LN_EPS) * g + b


LN_BLOCK_ROWS = 64


def _zero_after(v):
    bits = pltpu.bitcast(v[0:V7X_SUBLANES, 0:128], jnp.uint32)
    return pltpu.bitcast((bits >> 16) >> 16, F32)


def _residual_ln_block(x_ref, acc_ref, g_ref, b_ref, o_ref, r0, acc_r0, *, alpha, scale, after=None):
    rows = slice(r0, r0 + LN_BLOCK_ROWS)
    y = alpha * x_ref[rows, :]
    if after is not None:
        y = y + jnp.tile(_zero_after(after), (LN_BLOCK_ROWS // V7X_SUBLANES, y.shape[1] // 128))
    acc = acc_ref[acc_r0:acc_r0 + LN_BLOCK_ROWS, :]
    y = y + (acc if scale == 1.0 else scale * acc)
    o_ref[rows, :] = _layer_norm(y, g_ref[...], b_ref[...])


def _ffn_kernel(x_ref, wg_ref, wu_ref, wd_ref, g_ref, b_ref, o_ref, acc_ref, *, alpha, sub_rows):
    d_ff = wg_ref.shape[1]
    c = V7X_MXU_DIM
    n_chunks = d_ff // c
    tm = x_ref.shape[0]
    ln_blocks = sub_rows // LN_BLOCK_ROWS
    assert sub_rows % LN_BLOCK_ROWS == 0 and ln_blocks <= n_chunks
    epilogue = functools.partial(_residual_ln_block, x_ref, acc_ref, g_ref, b_ref, o_ref,
                                 alpha=alpha, scale=0.5)

    for s0 in range(0, tm, sub_rows):
        xb = x_ref[s0:s0 + sub_rows, :].astype(BF16)
        for k in range(n_chunks):
            lo = k * c
            gate = _dot(xb, wg_ref[:, lo:lo + c])
            up = _dot(xb, wu_ref[:, lo:lo + c])
            h = (jax.nn.silu(gate) * up).astype(BF16)
            d = _dot(h, wd_ref[lo:lo + c, :])
            acc = d if k == 0 else acc + d
            if s0 > 0 and k < ln_blocks:
                epilogue(s0 - sub_rows + k * LN_BLOCK_ROWS, k * LN_BLOCK_ROWS, after=d)
        acc_ref[...] = acc
    for k in range(ln_blocks):
        epilogue(tm - sub_rows + k * LN_BLOCK_ROWS, k * LN_BLOCK_ROWS)


def _gmlp_kernel(x_ref, win_ref, bin_ref, lng_ref, lnb_ref, ws_ref, bst_ref, wout_ref,
                 g_ref, b_ref, o_ref, uf_ref, vf_ref, acc_ref, *, alpha, sub_rows):
    e = vf_ref.shape[1]
    gd = e // GMLP_GROUPS
    p = GMLP_CHUNK
    causal = (lax.broadcasted_iota(jnp.int32, (p, p), 0)
              >= lax.broadcasted_iota(jnp.int32, (p, p), 1))
    tm = x_ref.shape[0]
    ln_blocks = sub_rows // LN_BLOCK_ROWS
    assert sub_rows % LN_BLOCK_ROWS == 0 and ln_blocks <= GMLP_GROUPS
    epilogue = functools.partial(_residual_ln_block, x_ref, acc_ref, g_ref, b_ref, o_ref,
                                 alpha=alpha, scale=1.0)
    for r0 in range(0, tm, sub_rows):
        rows = slice(r0, r0 + sub_rows)
        xb = x_ref[rows, :].astype(BF16)

        vsum = jnp.zeros((sub_rows, 1), F32)
        for g in range(GMLP_GROUPS):
            cols = slice(g * gd, (g + 1) * gd)
            vcols = slice(e + g * gd, e + (g + 1) * gd)
            v = jax.nn.gelu(_dot(xb, win_ref[:, vcols]) + bin_ref[:, vcols])
            vf_ref[rows, cols] = v
            vsum = vsum + jnp.sum(v, axis=-1, keepdims=True)

        mean = vsum / e
        dv = vf_ref[rows, :] - mean
        rstd = lax.rsqrt(jnp.mean(dv * dv, axis=-1, keepdims=True) + LN_EPS)

        for g in range(GMLP_GROUPS):
            cols = slice(g * gd, (g + 1) * gd)
            uf_ref[rows, cols] = jax.nn.gelu(_dot(xb, win_ref[:, cols]) + bin_ref[:, cols])

        for g in range(GMLP_GROUPS):
            cols = slice(g * gd, (g + 1) * gd)
            vn = ((vf_ref[rows, cols] - mean) * rstd * lng_ref[:, cols] + lnb_ref[:, cols]).astype(BF16)
            ws = jnp.where(causal, ws_ref[g], jnp.zeros_like(ws_ref[g]))
            bs = bst_ref[:, g:g + 1]
            parts = []
            for n in range(sub_rows // p):
                sv = _dot(ws, vn[n * p:(n + 1) * p, :]) + bs
                parts.append(uf_ref[r0 + n * p:r0 + (n + 1) * p, cols] * sv)
            gated = jnp.concatenate(parts, axis=0).astype(BF16)
            d = _dot(gated, wout_ref[cols, :])
            acc = d if g == 0 else acc + d
            if r0 > 0 and g < ln_blocks:
                epilogue(r0 - sub_rows + g * LN_BLOCK_ROWS, g * LN_BLOCK_ROWS, after=d)
        acc_ref[...] = acc
    for k in range(ln_blocks):
        epilogue(tm - sub_rows + k * LN_BLOCK_ROWS, k * LN_BLOCK_ROWS)


CONV_HALO = 32
CONV_ROW_BLOCK = 32


def _depthwise_block(hs_ref, wb_ref, conv_ref, r0):
    sub = V7X_SUBLANES
    groups = range(CONV_ROW_BLOCK // sub)
    acc = [wb_ref[CONV_WIDTH] for _ in groups]
    for r in range(sub):
        for a in range((CONV_WIDTH - 1 - r) // sub + 1):
            w = wb_ref[CONV_WIDTH - 1 - (sub * a + r)]
            for q in groups:
                rows = hs_ref[r, pl.ds(r0 + (CONV_HALO - sub * a + sub * q), sub), :]
                acc[q] = acc[q] + rows * w
    for q in groups:
        conv_ref[pl.ds(r0 + sub * q, sub), :] = acc[q]


def _conv_kernel(x_ref, w1_ref, b1_ref, wdw_ref, bdw_ref, lng_ref, lnb_ref, w2_ref, b2_ref,
                 g_ref, b_ref, o_ref, hs_ref, wb_ref, conv_ref, *, alpha, tiles_per_seq):
    tm = x_ref.shape[0]
    c = conv_ref.shape[1]
    sub = V7X_SUBLANES
    i = pl.program_id(0)

    @pl.when(i % tiles_per_seq == 0)
    def _():
        for r in range(sub):
            hs_ref[r, 0:CONV_HALO + r, :] = jnp.zeros((CONV_HALO + r, c), F32)

    for k in range(CONV_WIDTH):
        wb_ref[k] = jnp.broadcast_to(wdw_ref[k:k + 1, :], (sub, c))
    wb_ref[CONV_WIDTH] = jnp.broadcast_to(bdw_ref[...], (sub, c))

    x = x_ref[...]
    xb = x.astype(BF16)
    a = _dot(xb, w1_ref[:, :c]) + b1_ref[:, :c]
    gate = _dot(xb, w1_ref[:, c:]) + b1_ref[:, c:]
    glu = a * jax.nn.sigmoid(gate)
    for r in range(sub):
        hs_ref[r, CONV_HALO + r:CONV_HALO + r + tm, :] = glu

    def conv_rows(j, carry):
        _depthwise_block(hs_ref, wb_ref, conv_ref, pl.multiple_of(j * CONV_ROW_BLOCK, CONV_ROW_BLOCK))
        return carry

    lax.fori_loop(0, tm // CONV_ROW_BLOCK, conv_rows, 0)
    for r in range(sub):
        hs_ref[r, 0:CONV_HALO + r, :] = hs_ref[r, tm:tm + CONV_HALO + r, :]

    hc = jax.nn.silu(_layer_norm(conv_ref[...], lng_ref[...], lnb_ref[...]))
    out = _dot(hc.astype(BF16), w2_ref[...]) + b2_ref[...]
    y = alpha * x + out
    o_ref[...] = _layer_norm(y, g_ref[...], b_ref[...])


POOL_HALO = 16


def _pool_kernel(x_ref, win_ref, wgrp_ref, scale_ref, wout_ref, g_ref, b_ref, o_ref,
                 hbuf_ref, *, alpha, tiles_per_seq):
    tm = x_ref.shape[0]
    dp = hbuf_ref.shape[1]
    gd = dp // len(POOL_WINDOWS)
    i = pl.program_id(0)

    @pl.when(i % tiles_per_seq == 0)
    def _():
        hbuf_ref[0:POOL_HALO, :] = jnp.zeros((POOL_HALO, dp), F32)

    x = x_ref[...]
    xb = x.astype(BF16)
    hbuf_ref[POOL_HALO:POOL_HALO + tm, :] = _dot(xb, win_ref[...])

    pos = (i % tiles_per_seq) * tm + 1 + lax.broadcasted_iota(jnp.int32, (tm, gd), 0)
    ys = []
    for gi, win in enumerate(POOL_WINDOWS):
        lo = gi * gd
        rows = tm + POOL_HALO
        s = hbuf_ref[:, lo:lo + gd]
        w = 1
        while w < win:
            s = s[w:, :] + s[:rows - w, :]
            rows -= w
            w *= 2
        s = s[rows - tm:, :]
        h = hbuf_ref[POOL_HALO:POOL_HALO + tm, lo:lo + gd]
        count = jnp.minimum(pos, win).astype(F32)
        pooled = (s / count - h).astype(BF16)
        ys.append(_dot(pooled, wgrp_ref[gi]))
    hbuf_ref[0:POOL_HALO, :] = hbuf_ref[tm:tm + POOL_HALO, :]
    y = (jnp.concatenate(ys, axis=1) * scale_ref[...]).astype(BF16)
    out = _dot(y, wout_ref[...])
    o_ref[...] = _layer_norm(alpha * x + out, g_ref[...], b_ref[...])


def _resident(arr):
    zeros = (0,) * arr.ndim
    return pl.BlockSpec(arr.shape, lambda i: zeros, pipeline_mode=pl.Buffered(1))


def _token_tiled_call(kernel_fn, x2, params, scratch_shapes, *, tm, semantics, name):
    n_tok, d = x2.shape
    tile = pl.BlockSpec((tm, d), lambda i: (i, 0))
    return pl.pallas_call(
        kernel_fn,
        out_shape=jax.ShapeDtypeStruct((n_tok, d), x2.dtype),
        grid=(n_tok // tm,),
        in_specs=[tile] + [_resident(p) for p in params],
        out_specs=tile,
        scratch_shapes=scratch_shapes,
        compiler_params=pltpu.CompilerParams(
            dimension_semantics=(semantics,),
            vmem_limit_bytes=V7X_VMEM_BYTES * 7 // 8),
        name=name,
    )(x2, *params)


def _row(v):
    return v.reshape(1, -1).astype(F32)


def _pick_tile(seq, target):
    tm = min(seq, target)
    assert seq % tm == 0 and tm % GMLP_CHUNK == 0, (seq, tm)
    return tm


FFN_TILE = 2048
FFN_SUB_TILE = 512
GMLP_TILE = 1024
GMLP_SUB_TILE = 512
MIXER_TILE = 512


def _forward(x, ln_g, ln_b, ffn_w_gate, ffn_w_up, ffn_w_down,
             a_w_in, a_b_in, a_ln_g, a_ln_b, a_w_s, a_b_s, a_w_out,
             b_w_pw1, b_b_pw1, b_w_dw, b_b_dw, b_ln_g, b_ln_b, b_w_pw2, b_b_pw2,
             c_w_in, c_w_grp, c_scale, c_w_out,
             *, ffn_tile=FFN_TILE, gmlp_tile=GMLP_TILE, mixer_tile=MIXER_TILE):
    bsz, seq, d = x.shape
    depth = ln_g.shape[0]
    d_ff = ffn_w_gate.shape[-1]
    alpha = float((2 * depth) ** 0.25)
    tm = _pick_tile(seq, mixer_tile)
    tm_ffn = _pick_tile(seq, ffn_tile)
    tm_gmlp = _pick_tile(seq, gmlp_tile)
    tiles_per_seq = seq // tm
    x2 = x.reshape(bsz * seq, d)

    assert d_ff % V7X_MXU_DIM == 0, d_ff

    def ffn(x2, i, s):
        return _token_tiled_call(
            functools.partial(_ffn_kernel, alpha=alpha, sub_rows=min(tm_ffn, FFN_SUB_TILE)), x2,
            [ffn_w_gate[i, s].astype(BF16), ffn_w_up[i, s].astype(BF16),
             ffn_w_down[i, s].astype(BF16), _row(ln_g[i, 2 * s]), _row(ln_b[i, 2 * s])],
            [pltpu.VMEM((min(tm_ffn, FFN_SUB_TILE), d), F32)],
            tm=tm_ffn, semantics="parallel", name="ffn")

    for i in range(depth):
        m, j = i % N_MIXERS, i // N_MIXERS
        x2 = ffn(x2, i, 0)
        g1, b1 = _row(ln_g[i, 1]), _row(ln_b[i, 1])
        if m == 0:
            e = a_w_out.shape[1]
            x2 = _token_tiled_call(
                functools.partial(_gmlp_kernel, alpha=alpha,
                                  sub_rows=min(tm_gmlp, GMLP_SUB_TILE)), x2,
                [a_w_in[j].astype(BF16), _row(a_b_in[j]), _row(a_ln_g[j]), _row(a_ln_b[j]),
                 a_w_s[j].astype(BF16), a_b_s[j].T.astype(F32), a_w_out[j].astype(BF16), g1, b1],
                [pltpu.VMEM((tm_gmlp, e), F32), pltpu.VMEM((tm_gmlp, e), F32),
                 pltpu.VMEM((min(tm_gmlp, GMLP_SUB_TILE), d), F32)],
                tm=tm_gmlp, semantics="parallel", name="gmlp")
        elif m == 1:
            c = b_w_dw.shape[-1]
            x2 = _token_tiled_call(
                functools.partial(_conv_kernel, alpha=alpha, tiles_per_seq=tiles_per_seq), x2,
                [b_w_pw1[j].astype(BF16), _row(b_b_pw1[j]), b_w_dw[j].astype(F32), _row(b_b_dw[j]),
                 _row(b_ln_g[j]), _row(b_ln_b[j]), b_w_pw2[j].astype(BF16), _row(b_b_pw2[j]), g1, b1],
                [pltpu.VMEM((V7X_SUBLANES, tm + CONV_HALO + V7X_SUBLANES, c), F32),
                 pltpu.VMEM((CONV_WIDTH + 1, V7X_SUBLANES, c), F32), pltpu.VMEM((tm, c), F32)],
                tm=tm, semantics="arbitrary", name="conv")
        else:
            dp = c_w_in.shape[-1]
            x2 = _token_tiled_call(
                functools.partial(_pool_kernel, alpha=alpha, tiles_per_seq=tiles_per_seq), x2,
                [c_w_in[j].astype(BF16), c_w_grp[j].astype(BF16), _row(c_scale[j]),
                 c_w_out[j].astype(BF16), g1, b1],
                [pltpu.VMEM((tm + POOL_HALO, dp), F32)],
                tm=tm, semantics="arbitrary", name="pool")
        x2 = ffn(x2, i, 1)
    return x2.reshape(bsz, seq, d)


def kernel(x, ln_g, ln_b, ffn_w_gate, ffn_w_up, ffn_w_down, a_w_in, a_b_in, a_ln_g, a_ln_b, a_w_s, a_b_s, a_w_out, b_w_pw1, b_b_pw1, b_w_dw, b_b_dw, b_ln_g, b_ln_b, b_w_pw2, b_b_pw2, c_w_in, c_w_grp, c_scale, c_w_out):
    return _forward(x, ln_g, ln_b, ffn_w_gate, ffn_w_up, ffn_w_down,
                    a_w_in, a_b_in, a_ln_g, a_ln_b, a_w_s, a_b_s, a_w_out,
                    b_w_pw1, b_b_pw1, b_w_dw, b_b_dw, b_ln_g, b_ln_b, b_w_pw2, b_b_pw2,
                    c_w_in, c_w_grp, c_scale, c_w_out)
```

```python
import functools

import jax
import jax.numpy as jnp
from jax import lax
from jax.experimental import pallas as pl
from jax.experimental.pallas import tpu as pltpu

N_MIXERS = 3
LN_EPS = 1e-5
GMLP_CHUNK = 128
GMLP_GROUPS = 8
CONV_WIDTH = 31
POOL_WINDOWS = (2, 4, 8, 16)

V7X_SUBLANES = 8
V7X_MXU_DIM = 256
V7X_VMEM_BYTES = 64 * 1024 * 1024

BF16 = jnp.bfloat16
F32 = jnp.float32


def _dot(a, b):
    return jnp.dot(a, b, preferred_element_type=F32)


def _layer_norm(y, g, b):
    mean = jnp.mean(y, axis=-1, keepdims=True)
    d = y - mean
    var = jnp.mean(d * d, axis=-1, keepdims=True)
    return d * lax.rsqrt(var + LN_EPS) * g + b


LN_BLOCK_ROWS = 64


def _zero_after(v):
    bits = pltpu.bitcast(v[0:V7X_SUBLANES, 0:128], jnp.uint32)
    return pltpu.bitcast((bits >> 16) >> 16, F32)


def _residual_ln_block(x_ref, acc_ref, g_ref, b_ref, o_ref, r0, acc_r0, *, alpha, scale, after=None):
    rows = slice(r0, r0 + LN_BLOCK_ROWS)
    y = alpha * x_ref[rows, :]
    if after is not None:
        y = y + jnp.tile(_zero_after(after), (LN_BLOCK_ROWS // V7X_SUBLANES, y.shape[1] // 128))
    acc = acc_ref[acc_r0:acc_r0 + LN_BLOCK_ROWS, :]
    y = y + (acc if scale == 1.0 else scale * acc)
    o_ref[rows, :] = _layer_norm(y, g_ref[...], b_ref[...])


def _ffn_kernel(x_ref, wg_ref, wu_ref, wd_ref, g_ref, b_ref, o_ref, acc_ref, *, alpha, sub_rows):
    d_ff = wg_ref.shape[1]
    c = V7X_MXU_DIM
    n_chunks = d_ff // c
    tm = x_ref.shape[0]
    ln_blocks = sub_rows // LN_BLOCK_ROWS
    assert sub_rows % LN_BLOCK_ROWS == 0 and ln_blocks <= n_chunks
    epilogue = functools.partial(_residual_ln_block, x_ref, acc_ref, g_ref, b_ref, o_ref,
                                 alpha=alpha, scale=0.5)

    for s0 in range(0, tm, sub_rows):
        xb = x_ref[s0:s0 + sub_rows, :].astype(BF16)
        for k in range(n_chunks):
            lo = k * c
            gate = _dot(xb, wg_ref[:, lo:lo + c])
            up = _dot(xb, wu_ref[:, lo:lo + c])
            h = (jax.nn.silu(gate) * up).astype(BF16)
            d = _dot(h, wd_ref[lo:lo + c, :])
            acc = d if k == 0 else acc + d
            if s0 > 0 and k < ln_blocks:
                epilogue(s0 - sub_rows + k * LN_BLOCK_ROWS, k * LN_BLOCK_ROWS, after=d)
        acc_ref[...] = acc
    for k in range(ln_blocks):
        epilogue(tm - sub_rows + k * LN_BLOCK_ROWS, k * LN_BLOCK_ROWS)


def _gmlp_kernel(x_ref, win_ref, bin_ref, lng_ref, lnb_ref, ws_ref, bst_ref, wout_ref,
                 g_ref, b_ref, o_ref, uf_ref, vf_ref, acc_ref, *, alpha, sub_rows):
    e = vf_ref.shape[1]
    gd = e // GMLP_GROUPS
    p = GMLP_CHUNK
    causal = (lax.broadcasted_iota(jnp.int32, (p, p), 0)
              >= lax.broadcasted_iota(jnp.int32, (p, p), 1))
    tm = x_ref.shape[0]
    ln_blocks = sub_rows // LN_BLOCK_ROWS
    assert sub_rows % LN_BLOCK_ROWS == 0 and ln_blocks <= GMLP_GROUPS
    epilogue = functools.partial(_residual_ln_block, x_ref, acc_ref, g_ref, b_ref, o_ref,
                                 alpha=alpha, scale=1.0)
    for r0 in range(0, tm, sub_rows):
        rows = slice(r0, r0 + sub_rows)
        xb = x_ref[rows, :].astype(BF16)

        vsum = jnp.zeros((sub_rows, 1), F32)
        for g in range(GMLP_GROUPS):
            cols = slice(g * gd, (g + 1) * gd)
            vcols = slice(e + g * gd, e + (g + 1) * gd)
            v = jax.nn.gelu(_dot(xb, win_ref[:, vcols]) + bin_ref[:, vcols])
            vf_ref[rows, cols] = v
            vsum = vsum + jnp.sum(v, axis=-1, keepdims=True)

        mean = vsum / e
        dv = vf_ref[rows, :] - mean
        rstd = lax.rsqrt(jnp.mean(dv * dv, axis=-1, keepdims=True) + LN_EPS)

        for g in range(GMLP_GROUPS):
            cols = slice(g * gd, (g + 1) * gd)
            uf_ref[rows, cols] = jax.nn.gelu(_dot(xb, win_ref[:, cols]) + bin_ref[:, cols])

        for g in range(GMLP_GROUPS):
            cols = slice(g * gd, (g + 1) * gd)
            vn = ((vf_ref[rows, cols] - mean) * rstd * lng_ref[:, cols] + lnb_ref[:, cols]).astype(BF16)
            ws = jnp.where(causal, ws_ref[g], jnp.zeros_like(ws_ref[g]))
            bs = bst_ref[:, g:g + 1]
            parts = []
            for n in range(sub_rows // p):
                sv = _dot(ws, vn[n * p:(n + 1) * p, :]) + bs
                parts.append(uf_ref[r0 + n * p:r0 + (n + 1) * p, cols] * sv)
            gated = jnp.concatenate(parts, axis=0).astype(BF16)
            d = _dot(gated, wout_ref[cols, :])
            acc = d if g == 0 else acc + d
            if r0 > 0 and g < ln_blocks:
                epilogue(r0 - sub_rows + g * LN_BLOCK_ROWS, g * LN_BLOCK_ROWS, after=d)
        acc_ref[...] = acc
    for k in range(ln_blocks):
        epilogue(tm - sub_rows + k * LN_BLOCK_ROWS, k * LN_BLOCK_ROWS)


CONV_HALO = 32
CONV_ROW_BLOCK = 32


def _depthwise_block(hs_ref, wb_ref, conv_ref, r0):
    sub = V7X_SUBLANES
    groups = range(CONV_ROW_BLOCK // sub)
    acc = [wb_ref[CONV_WIDTH] for _ in groups]
    for r in range(sub):
        for a in range((CONV_WIDTH - 1 - r) // sub + 1):
            w = wb_ref[CONV_WIDTH - 1 - (sub * a + r)]
            for q in groups:
                rows = hs_ref[r, pl.ds(r0 + (CONV_HALO - sub * a + sub * q), sub), :]
                acc[q] = acc[q] + rows * w
    for q in groups:
        conv_ref[pl.ds(r0 + sub * q, sub), :] = acc[q]


def _conv_kernel(x_ref, w1_ref, b1_ref, wdw_ref, bdw_ref, lng_ref, lnb_ref, w2_ref, b2_ref,
                 g_ref, b_ref, o_ref, hs_ref, wb_ref, conv_ref, *, alpha, tiles_per_seq):
    tm = x_ref.shape[0]
    c = conv_ref.shape[1]
    sub = V7X_SUBLANES
    i = pl.program_id(0)

    @pl.when(i % tiles_per_seq == 0)
    def _():
        for r in range(sub):
            hs_ref[r, 0:CONV_HALO + r, :] = jnp.zeros((CONV_HALO + r, c), F32)

    for k in range(CONV_WIDTH):
        wb_ref[k] = jnp.broadcast_to(wdw_ref[k:k + 1, :], (sub, c))
    wb_ref[CONV_WIDTH] = jnp.broadcast_to(bdw_ref[...], (sub, c))

    xb = x_ref[...].astype(BF16)
    for lo in range(0, c, V7X_MXU_DIM):
        cols = slice(lo, lo + V7X_MXU_DIM)
        gcols = slice(c + lo, c + lo + V7X_MXU_DIM)
        a = _dot(xb, w1_ref[:, cols]) + b1_ref[:, cols]
        gate = _dot(xb, w1_ref[:, gcols]) + b1_ref[:, gcols]
        glu = a * jax.nn.sigmoid(gate)
        for r in range(sub):
            hs_ref[r, CONV_HALO + r:CONV_HALO + r + tm, cols] = glu

    def conv_rows(j, carry):
        r0 = pl.multiple_of(j * (2 * CONV_ROW_BLOCK), 2 * CONV_ROW_BLOCK)
        _depthwise_block(hs_ref, wb_ref, conv_ref, r0)
        _depthwise_block(hs_ref, wb_ref, conv_ref, r0 + CONV_ROW_BLOCK)
        return carry

    lax.fori_loop(0, tm // (2 * CONV_ROW_BLOCK), conv_rows, 0)
    for r in range(sub):
        hs_ref[r, 0:CONV_HALO + r, :] = hs_ref[r, tm:tm + CONV_HALO + r, :]

    sub_rows = tm // 2
    for s0 in range(0, tm, sub_rows):
        rows = slice(s0, s0 + sub_rows)
        hc = jax.nn.silu(_layer_norm(conv_ref[rows, :], lng_ref[...], lnb_ref[...]))
        out = _dot(hc.astype(BF16), w2_ref[...]) + b2_ref[...]
        y = alpha * x_ref[rows, :] + out
        o_ref[rows, :] = _layer_norm(y, g_ref[...], b_ref[...])


POOL_HALO = 16


def _pool_kernel(x_ref, win_ref, wgrp_ref, scale_ref, wout_ref, g_ref, b_ref, o_ref,
                 hbuf_ref, *, alpha, tiles_per_seq):
    tm = x_ref.shape[0]
    dp = hbuf_ref.shape[1]
    gd = dp // len(POOL_WINDOWS)
    i = pl.program_id(0)

    @pl.when(i % tiles_per_seq == 0)
    def _():
        hbuf_ref[0:POOL_HALO, :] = jnp.zeros((POOL_HALO, dp), F32)

    hbuf_ref[POOL_HALO:POOL_HALO + tm, :] = _dot(x_ref[...].astype(BF16), win_ref[...])

    sub_rows = tm // 2
    for s0 in range(0, tm, sub_rows):
        pos = ((i % tiles_per_seq) * tm + s0 + 1
               + lax.broadcasted_iota(jnp.int32, (sub_rows, gd), 0))
        ys = []
        for gi, win in enumerate(POOL_WINDOWS):
            lo = gi * gd
            rows = sub_rows + POOL_HALO
            s = hbuf_ref[s0:s0 + rows, lo:lo + gd]
            w = 1
            while w < win:
                s = s[w:, :] + s[:rows - w, :]
                rows -= w
                w *= 2
            s = s[rows - sub_rows:, :]
            h = hbuf_ref[POOL_HALO + s0:POOL_HALO + s0 + sub_rows, lo:lo + gd]
            count = jnp.minimum(pos, win).astype(F32)
            pooled = (s / count - h).astype(BF16)
            ys.append(_dot(pooled, wgrp_ref[gi]))
        y = (jnp.concatenate(ys, axis=1) * scale_ref[...]).astype(BF16)
        out = _dot(y, wout_ref[...])
        o_ref[s0:s0 + sub_rows, :] = _layer_norm(alpha * x_ref[s0:s0 + sub_rows, :] + out,
                                                 g_ref[...], b_ref[...])
    hbuf_ref[0:POOL_HALO, :] = hbuf_ref[tm:tm + POOL_HALO, :]


def _resident(arr):
    zeros = (0,) * arr.ndim
    return pl.BlockSpec(arr.shape, lambda i: zeros, pipeline_mode=pl.Buffered(1))


def _token_tiled_call(kernel_fn, x2, params, scratch_shapes, *, tm, semantics, name):
    n_tok, d = x2.shape
    tile = pl.BlockSpec((tm, d), lambda i: (i, 0))
    return pl.pallas_call(
        kernel_fn,
        out_shape=jax.ShapeDtypeStruct((n_tok, d), x2.dtype),
        grid=(n_tok // tm,),
        in_specs=[tile] + [_resident(p) for p in params],
        out_specs=tile,
        scratch_shapes=scratch_shapes,
        compiler_params=pltpu.CompilerParams(
            dimension_semantics=(semantics,),
            vmem_limit_bytes=V7X_VMEM_BYTES * 7 // 8),
        name=name,
    )(x2, *params)


def _row(v):
    return v.reshape(1, -1).astype(F32)


def _pick_tile(seq, target):
    tm = min(seq, target)
    assert seq % tm == 0 and tm % GMLP_CHUNK == 0, (seq, tm)
    return tm


FFN_TILE = 2048
FFN_SUB_TILE = 512
GMLP_TILE = 1024
GMLP_SUB_TILE = 512
MIXER_TILE = 512


def _forward(x, ln_g, ln_b, ffn_w_gate, ffn_w_up, ffn_w_down,
             a_w_in, a_b_in, a_ln_g, a_ln_b, a_w_s, a_b_s, a_w_out,
             b_w_pw1, b_b_pw1, b_w_dw, b_b_dw, b_ln_g, b_ln_b, b_w_pw2, b_b_pw2,
             c_w_in, c_w_grp, c_scale, c_w_out,
             *, ffn_tile=FFN_TILE, gmlp_tile=GMLP_TILE, mixer_tile=MIXER_TILE):
    bsz, seq, d = x.shape
    depth = ln_g.shape[0]
    d_ff = ffn_w_gate.shape[-1]
    alpha = float((2 * depth) ** 0.25)
    tm = _pick_tile(seq, mixer_tile)
    tm_ffn = _pick_tile(seq, ffn_tile)
    tm_gmlp = _pick_tile(seq, gmlp_tile)
    tiles_per_seq = seq // tm
    x2 = x.reshape(bsz * seq, d)

    assert d_ff % V7X_MXU_DIM == 0, d_ff

    def ffn(x2, i, s):
        return _token_tiled_call(
            functools.partial(_ffn_kernel, alpha=alpha, sub_rows=min(tm_ffn, FFN_SUB_TILE)), x2,
            [ffn_w_gate[i, s].astype(BF16), ffn_w_up[i, s].astype(BF16),
             ffn_w_down[i, s].astype(BF16), _row(ln_g[i, 2 * s]), _row(ln_b[i, 2 * s])],
            [pltpu.VMEM((min(tm_ffn, FFN_SUB_TILE), d), F32)],
            tm=tm_ffn, semantics="parallel", name="ffn")

    for i in range(depth):
        m, j = i % N_MIXERS, i // N_MIXERS
        x2 = ffn(x2, i, 0)
        g1, b1 = _row(ln_g[i, 1]), _row(ln_b[i, 1])
        if m == 0:
            e = a_w_out.shape[1]
            x2 = _token_tiled_call(
                functools.partial(_gmlp_kernel, alpha=alpha,
                                  sub_rows=min(tm_gmlp, GMLP_SUB_TILE)), x2,
                [a_w_in[j].astype(BF16), _row(a_b_in[j]), _row(a_ln_g[j]), _row(a_ln_b[j]),
                 a_w_s[j].astype(BF16), a_b_s[j].T.astype(F32), a_w_out[j].astype(BF16), g1, b1],
                [pltpu.VMEM((tm_gmlp, e), F32), pltpu.VMEM((tm_gmlp, e), F32),
                 pltpu.VMEM((min(tm_gmlp, GMLP_SUB_TILE), d), F32)],
                tm=tm_gmlp, semantics="parallel", name="gmlp")
        elif m == 1:
            c = b_w_dw.shape[-1]
            x2 = _token_tiled_call(
                functools.partial(_conv_kernel, alpha=alpha, tiles_per_seq=tiles_per_seq), x2,
                [b_w_pw1[j].astype(BF16), _row(b_b_pw1[j]), b_w_dw[j].astype(F32), _row(b_b_dw[j]),
                 _row(b_ln_g[j]), _row(b_ln_b[j]), b_w_pw2[j].astype(BF16), _row(b_b_pw2[j]), g1, b1],
                [pltpu.VMEM((V7X_SUBLANES, tm + CONV_HALO + V7X_SUBLANES, c), F32),
                 pltpu.VMEM((CONV_WIDTH + 1, V7X_SUBLANES, c), F32), pltpu.VMEM((tm, c), F32)],
                tm=tm, semantics="arbitrary", name="conv")
        else:
            dp = c_w_in.shape[-1]
            x2 = _token_tiled_call(
                functools.partial(_pool_kernel, alpha=alpha, tiles_per_seq=tiles_per_seq), x2,
                [c_w_in[j].astype(BF16), c_w_grp[j].astype(BF16), _row(c_scale[j]),
                 c_w_out[j].astype(BF16), g1, b1],
                [pltpu.VMEM((tm + POOL_HALO, dp), F32)],
                tm=tm, semantics="arbitrary", name="pool")
        x2 = ffn(x2, i, 1)
    return x2.reshape(bsz, seq, d)


def kernel(x, ln_g, ln_b, ffn_w_gate, ffn_w_up, ffn_w_down, a_w_in, a_b_in, a_ln_g, a_ln_b, a_w_s, a_b_s, a_w_out, b_w_pw1, b_b_pw1, b_w_dw, b_b_dw, b_ln_g, b_ln_b, b_w_pw2, b_b_pw2, c_w_in, c_w_grp, c_scale, c_w_out):
    return _forward(x, ln_g, ln_b, ffn_w_gate, ffn_w_up, ffn_w_down,
                    a_w_in, a_b_in, a_ln_g, a_ln_b, a_w_s, a_b_s, a_w_out,
                    b_w_pw1, b_b_pw1, b_w_dw, b_b_dw, b_ln_g, b_ln_b, b_w_pw2, b_b_pw2,
                    c_w_in, c_w_grp, c_scale, c_w_out)
```

```python
import functools

import jax
import jax.numpy as jnp
from jax import lax
from jax.experimental import pallas as pl
from jax.experimental.pallas import tpu as pltpu

N_MIXERS = 3
LN_EPS = 1e-5
GMLP_CHUNK = 128
GMLP_GROUPS = 8
CONV_WIDTH = 31
POOL_WINDOWS = (2, 4, 8, 16)

V7X_SUBLANES = 8
V7X_LANES = 128
V7X_MXU_DIM = 256
V7X_VMEM_BYTES = 64 * 1024 * 1024

BF16 = jnp.bfloat16
F32 = jnp.float32


def _dot(a, b):
    return jnp.dot(a, b, preferred_element_type=F32)


def _layer_norm(y, g, b):
    mean = jnp.mean(y, axis=-1, keepdims=True)
    d = y - mean
    var = jnp.mean(d * d, axis=-1, keepdims=True)
    return d * lax.rsqrt(var + LN_EPS) * g + b


LN_BLOCK_ROWS = 64


def _zero_after(v):
    bits = pltpu.bitcast(v[0:V7X_SUBLANES, 0:V7X_LANES], jnp.uint32)
    return pltpu.bitcast((bits >> 16) >> 16, F32)


def _residual_ln_block(x_ref, acc_ref, g_ref, b_ref, o_ref, r0, acc_r0, *, alpha, scale, after=None):
    rows = slice(r0, r0 + LN_BLOCK_ROWS)
    y = alpha * x_ref[rows, :]
    if after is not None:
        y = y + jnp.tile(_zero_after(after), (LN_BLOCK_ROWS // V7X_SUBLANES, y.shape[1] // V7X_LANES))
    acc = acc_ref[acc_r0:acc_r0 + LN_BLOCK_ROWS, :]
    y = y + (acc if scale == 1.0 else scale * acc)
    o_ref[rows, :] = _layer_norm(y, g_ref[...], b_ref[...])


def _ffn_kernel(x_ref, wg_ref, wu_ref, wd_ref, g_ref, b_ref, o_ref, acc_ref, *, alpha, sub_rows):
    d_ff = wg_ref.shape[1]
    c = V7X_MXU_DIM
    n_chunks = d_ff // c
    tm = x_ref.shape[0]
    ln_blocks = sub_rows // LN_BLOCK_ROWS
    assert sub_rows % LN_BLOCK_ROWS == 0 and ln_blocks <= n_chunks
    epilogue = functools.partial(_residual_ln_block, x_ref, acc_ref, g_ref, b_ref, o_ref,
                                 alpha=alpha, scale=0.5)

    for s0 in range(0, tm, sub_rows):
        xb = x_ref[s0:s0 + sub_rows, :].astype(BF16)
        for k in range(n_chunks):
            lo = k * c
            gate = _dot(xb, wg_ref[:, lo:lo + c])
            up = _dot(xb, wu_ref[:, lo:lo + c])
            h = (jax.nn.silu(gate) * up).astype(BF16)
            d = _dot(h, wd_ref[lo:lo + c, :])
            acc = d if k == 0 else acc + d
            if s0 > 0 and k < ln_blocks:
                epilogue(s0 - sub_rows + k * LN_BLOCK_ROWS, k * LN_BLOCK_ROWS, after=d)
        acc_ref[...] = acc
    for k in range(ln_blocks):
        epilogue(tm - sub_rows + k * LN_BLOCK_ROWS, k * LN_BLOCK_ROWS)


def _gmlp_kernel(x_ref, win_ref, bin_ref, lng_ref, lnb_ref, ws_ref, bst_ref, wout_ref,
                 g_ref, b_ref, o_ref, uf_ref, vf_ref, acc_ref, *, alpha, sub_rows):
    e = wout_ref.shape[0]
    gd = e // GMLP_GROUPS
    p = GMLP_CHUNK
    causal = (lax.broadcasted_iota(jnp.int32, (p, p), 0)
              >= lax.broadcasted_iota(jnp.int32, (p, p), 1))
    tm = x_ref.shape[0]
    ln_blocks = sub_rows // LN_BLOCK_ROWS
    assert sub_rows % LN_BLOCK_ROWS == 0 and ln_blocks <= GMLP_GROUPS
    epilogue = functools.partial(_residual_ln_block, x_ref, acc_ref, g_ref, b_ref, o_ref,
                                 alpha=alpha, scale=1.0)
    for r0 in range(0, tm, sub_rows):
        rows = slice(r0, r0 + sub_rows)
        xb = x_ref[rows, :].astype(BF16)

        vsum = jnp.zeros((sub_rows, 1), F32)
        for g in range(GMLP_GROUPS):
            cols = slice(g * gd, (g + 1) * gd)
            vcols = slice(e + g * gd, e + (g + 1) * gd)
            v = jax.nn.gelu(_dot(xb, win_ref[:, vcols]) + bin_ref[:, vcols])
            vf_ref[rows, cols] = v
            vsum = vsum + jnp.sum(v, axis=-1, keepdims=True)

        mean = vsum / e
        dv = vf_ref[rows, 0:e] - mean
        rstd = lax.rsqrt(jnp.mean(dv * dv, axis=-1, keepdims=True) + LN_EPS)

        for g in range(GMLP_GROUPS):
            cols = slice(g * gd, (g + 1) * gd)
            uf_ref[rows, cols] = jax.nn.gelu(_dot(xb, win_ref[:, cols]) + bin_ref[:, cols])

        for g in range(GMLP_GROUPS):
            cols = slice(g * gd, (g + 1) * gd)
            vn = ((vf_ref[rows, cols] - mean) * rstd * lng_ref[:, cols] + lnb_ref[:, cols]).astype(BF16)
            ws = jnp.where(causal, ws_ref[g], jnp.zeros_like(ws_ref[g]))
            bs = bst_ref[:, g:g + 1]
            parts = []
            for n in range(sub_rows // p):
                sv = _dot(ws, vn[n * p:(n + 1) * p, :]) + bs
                parts.append(uf_ref[r0 + n * p:r0 + (n + 1) * p, cols] * sv)
            gated = jnp.concatenate(parts, axis=0).astype(BF16)
            d = _dot(gated, wout_ref[cols, :])
            acc = d if g == 0 else acc + d
            if r0 > 0 and g < ln_blocks:
                epilogue(r0 - sub_rows + g * LN_BLOCK_ROWS, g * LN_BLOCK_ROWS, after=d)
        acc_ref[...] = acc
    for k in range(ln_blocks):
        epilogue(tm - sub_rows + k * LN_BLOCK_ROWS, k * LN_BLOCK_ROWS)


CONV_HALO = 32
CONV_ROW_BLOCK = 32


def _depthwise_block(hs_ref, wb_ref, conv_ref, r0):
    sub = V7X_SUBLANES
    groups = range(CONV_ROW_BLOCK // sub)
    acc = [wb_ref[CONV_WIDTH] for _ in groups]
    for r in range(sub):
        for a in range((CONV_WIDTH - 1 - r) // sub + 1):
            w = wb_ref[CONV_WIDTH - 1 - (sub * a + r)]
            for q in groups:
                rows = hs_ref[r, pl.ds(r0 + (CONV_HALO - sub * a + sub * q), sub), :]
                acc[q] = acc[q] + rows * w
    for q in groups:
        conv_ref[pl.ds(r0 + sub * q, sub), :] = acc[q]


def _conv_kernel(x_ref, w1_ref, b1_ref, wdw_ref, bdw_ref, lng_ref, lnb_ref, w2_ref, b2_ref,
                 g_ref, b_ref, o_ref, hs_ref, wb_ref, conv_ref, *, alpha, tiles_per_seq):
    tm = x_ref.shape[0]
    c = conv_ref.shape[1]
    sub = V7X_SUBLANES
    i = pl.program_id(0)

    @pl.when(i % tiles_per_seq == 0)
    def _():
        for r in range(sub):
            hs_ref[r, 0:CONV_HALO + r, :] = jnp.zeros((CONV_HALO + r, c), F32)

    for k in range(CONV_WIDTH):
        wb_ref[k] = jnp.broadcast_to(wdw_ref[k:k + 1, :], (sub, c))
    wb_ref[CONV_WIDTH] = jnp.broadcast_to(bdw_ref[...], (sub, c))

    xb = x_ref[...].astype(BF16)
    for lo in range(0, c, V7X_MXU_DIM):
        cols = slice(lo, lo + V7X_MXU_DIM)
        gcols = slice(c + lo, c + lo + V7X_MXU_DIM)
        a = _dot(xb, w1_ref[:, cols]) + b1_ref[:, cols]
        gate = _dot(xb, w1_ref[:, gcols]) + b1_ref[:, gcols]
        glu = a * jax.nn.sigmoid(gate)
        for r in range(sub):
            hs_ref[r, CONV_HALO + r:CONV_HALO + r + tm, cols] = glu

    def conv_rows(j, carry):
        r0 = pl.multiple_of(j * (2 * CONV_ROW_BLOCK), 2 * CONV_ROW_BLOCK)
        _depthwise_block(hs_ref, wb_ref, conv_ref, r0)
        _depthwise_block(hs_ref, wb_ref, conv_ref, r0 + CONV_ROW_BLOCK)
        return carry

    lax.fori_loop(0, tm // (2 * CONV_ROW_BLOCK), conv_rows, 0)
    for r in range(sub):
        hs_ref[r, 0:CONV_HALO + r, :] = hs_ref[r, tm:tm + CONV_HALO + r, :]

    sub_rows = tm // 2
    for s0 in range(0, tm, sub_rows):
        rows = slice(s0, s0 + sub_rows)
        hc = jax.nn.silu(_layer_norm(conv_ref[rows, :], lng_ref[...], lnb_ref[...]))
        out = _dot(hc.astype(BF16), w2_ref[...]) + b2_ref[...]
        y = alpha * x_ref[rows, :] + out
        o_ref[rows, :] = _layer_norm(y, g_ref[...], b_ref[...])


POOL_HALO = 16


def _pool_kernel(x_ref, win_ref, wgrp_ref, scale_ref, wout_ref, g_ref, b_ref, o_ref,
                 hbuf_ref, *, alpha, tiles_per_seq):
    tm = x_ref.shape[0]
    dp = hbuf_ref.shape[1]
    gd = dp // len(POOL_WINDOWS)
    i = pl.program_id(0)

    @pl.when(i % tiles_per_seq == 0)
    def _():
        hbuf_ref[0:POOL_HALO, :] = jnp.zeros((POOL_HALO, dp), F32)

    hbuf_ref[POOL_HALO:POOL_HALO + tm, :] = _dot(x_ref[...].astype(BF16), win_ref[...])

    sub_rows = tm // 2
    for s0 in range(0, tm, sub_rows):
        pos = ((i % tiles_per_seq) * tm + s0 + 1
               + lax.broadcasted_iota(jnp.int32, (sub_rows, gd), 0))
        ys = []
        for gi, win in enumerate(POOL_WINDOWS):
            lo = gi * gd
            rows = sub_rows + POOL_HALO
            s = hbuf_ref[s0:s0 + rows, lo:lo + gd]
            w = 1
            while w < win:
                s = s[w:, :] + s[:rows - w, :]
                rows -= w
                w *= 2
            s = s[rows - sub_rows:, :]
            h = hbuf_ref[POOL_HALO + s0:POOL_HALO + s0 + sub_rows, lo:lo + gd]
            count = jnp.minimum(pos, win).astype(F32)
            pooled = (s / count - h).astype(BF16)
            ys.append(_dot(pooled, wgrp_ref[gi]))
        y = (jnp.concatenate(ys, axis=1) * scale_ref[...]).astype(BF16)
        out = _dot(y, wout_ref[...])
        o_ref[s0:s0 + sub_rows, :] = _layer_norm(alpha * x_ref[s0:s0 + sub_rows, :] + out,
                                                 g_ref[...], b_ref[...])
    hbuf_ref[0:POOL_HALO, :] = hbuf_ref[tm:tm + POOL_HALO, :]


def _resident(arr):
    zeros = (0,) * arr.ndim
    return pl.BlockSpec(arr.shape, lambda i: zeros, pipeline_mode=pl.Buffered(1))


def _token_tiled_call(kernel_fn, x2, params, scratch_shapes, *, tm, semantics, name):
    n_tok, d = x2.shape
    tile = pl.BlockSpec((tm, d), lambda i: (i, 0))
    return pl.pallas_call(
        kernel_fn,
        out_shape=jax.ShapeDtypeStruct((n_tok, d), x2.dtype),
        grid=(n_tok // tm,),
        in_specs=[tile] + [_resident(p) for p in params],
        out_specs=tile,
        scratch_shapes=scratch_shapes,
        compiler_params=pltpu.CompilerParams(
            dimension_semantics=(semantics,),
            vmem_limit_bytes=V7X_VMEM_BYTES * 7 // 8),
        name=name,
    )(x2, *params)


def _row(v):
    return v.reshape(1, -1).astype(F32)


def _pick_tile(seq, target):
    tm = min(seq, target)
    assert seq % tm == 0 and tm % GMLP_CHUNK == 0, (seq, tm)
    return tm


FFN_TILE = 2048
FFN_SUB_TILE = 512
GMLP_TILE = 1024
GMLP_SUB_TILE = 512
CONV_TILE = 512
POOL_TILE = 1024


def _forward(x, ln_g, ln_b, ffn_w_gate, ffn_w_up, ffn_w_down,
             a_w_in, a_b_in, a_ln_g, a_ln_b, a_w_s, a_b_s, a_w_out,
             b_w_pw1, b_b_pw1, b_w_dw, b_b_dw, b_ln_g, b_ln_b, b_w_pw2, b_b_pw2,
             c_w_in, c_w_grp, c_scale, c_w_out,
             *, ffn_tile=FFN_TILE, gmlp_tile=GMLP_TILE, conv_tile=CONV_TILE, pool_tile=POOL_TILE):
    bsz, seq, d = x.shape
    depth = ln_g.shape[0]
    d_ff = ffn_w_gate.shape[-1]
    alpha = float((2 * depth) ** 0.25)
    tm_conv = _pick_tile(seq, conv_tile)
    tm_pool = _pick_tile(seq, pool_tile)
    tm_ffn = _pick_tile(seq, ffn_tile)
    tm_gmlp = _pick_tile(seq, gmlp_tile)
    x2 = x.reshape(bsz * seq, d)

    assert d_ff % V7X_MXU_DIM == 0, d_ff

    def ffn(x2, i, s):
        return _token_tiled_call(
            functools.partial(_ffn_kernel, alpha=alpha, sub_rows=min(tm_ffn, FFN_SUB_TILE)), x2,
            [ffn_w_gate[i, s].astype(BF16), ffn_w_up[i, s].astype(BF16),
             ffn_w_down[i, s].astype(BF16), _row(ln_g[i, 2 * s]), _row(ln_b[i, 2 * s])],
            [pltpu.VMEM((min(tm_ffn, FFN_SUB_TILE), d), F32)],
            tm=tm_ffn, semantics="parallel", name="ffn")

    for i in range(depth):
        m, j = i % N_MIXERS, i // N_MIXERS
        x2 = ffn(x2, i, 0)
        g1, b1 = _row(ln_g[i, 1]), _row(ln_b[i, 1])
        if m == 0:
            e = a_w_out.shape[1]
            x2 = _token_tiled_call(
                functools.partial(_gmlp_kernel, alpha=alpha,
                                  sub_rows=min(tm_gmlp, GMLP_SUB_TILE)), x2,
                [a_w_in[j].astype(BF16), _row(a_b_in[j]), _row(a_ln_g[j]), _row(a_ln_b[j]),
                 a_w_s[j].astype(BF16), a_b_s[j].T.astype(F32), a_w_out[j].astype(BF16), g1, b1],
                [pltpu.VMEM((tm_gmlp, e + V7X_LANES), F32), pltpu.VMEM((tm_gmlp, e + V7X_LANES), F32),
                 pltpu.VMEM((min(tm_gmlp, GMLP_SUB_TILE), d), F32)],
                tm=tm_gmlp, semantics="parallel", name="gmlp")
        elif m == 1:
            c = b_w_dw.shape[-1]
            x2 = _token_tiled_call(
                functools.partial(_conv_kernel, alpha=alpha, tiles_per_seq=seq // tm_conv), x2,
                [b_w_pw1[j].astype(BF16), _row(b_b_pw1[j]), b_w_dw[j].astype(F32), _row(b_b_dw[j]),
                 _row(b_ln_g[j]), _row(b_ln_b[j]), b_w_pw2[j].astype(BF16), _row(b_b_pw2[j]), g1, b1],
                [pltpu.VMEM((V7X_SUBLANES, tm_conv + CONV_HALO + V7X_SUBLANES, c), F32),
                 pltpu.VMEM((CONV_WIDTH + 1, V7X_SUBLANES, c), F32), pltpu.VMEM((tm_conv, c), F32)],
                tm=tm_conv, semantics="arbitrary", name="conv")
        else:
            dp = c_w_in.shape[-1]
            x2 = _token_tiled_call(
                functools.partial(_pool_kernel, alpha=alpha, tiles_per_seq=seq // tm_pool), x2,
                [c_w_in[j].astype(BF16), c_w_grp[j].astype(BF16), _row(c_scale[j]),
                 c_w_out[j].astype(BF16), g1, b1],
                [pltpu.VMEM((tm_pool + POOL_HALO, dp), F32)],
                tm=tm_pool, semantics="arbitrary", name="pool")
        x2 = ffn(x2, i, 1)
    return x2.reshape(bsz, seq, d)


def kernel(x, ln_g, ln_b, ffn_w_gate, ffn_w_up, ffn_w_down, a_w_in, a_b_in, a_ln_g, a_ln_b, a_w_s, a_b_s, a_w_out, b_w_pw1, b_b_pw1, b_w_dw, b_b_dw, b_ln_g, b_ln_b, b_w_pw2, b_b_pw2, c_w_in, c_w_grp, c_scale, c_w_out):
    return _forward(x, ln_g, ln_b, ffn_w_gate, ffn_w_up, ffn_w_down,
                    a_w_in, a_b_in, a_ln_g, a_ln_b, a_w_s, a_b_s, a_w_out,
                    b_w_pw1, b_b_pw1, b_w_dw, b_b_dw, b_ln_g, b_ln_b, b_w_pw2, b_b_pw2,
                    c_w_in, c_w_grp, c_scale, c_w_out)
```
